```python
import functools
import jax, jax.numpy as jnp
from jax import lax
import numpy as np

D_MODEL = 2048
BATCH = 1
SEQ = 8192
DEPTH = 1
DEC_BATCH = 32
DEC_SEQ = 4
PAST_LEN = 8192
PAGE_SIZE = 128

D_MIX = D_MODEL
D_ATTN = D_MIX // 2
D_CONV = D_MIX - D_ATTN
HEAD_DIM = 64
N_HEADS = D_ATTN // HEAD_DIM
CONV_W = 3
Q_BLOCK = 128
N_EXPERTS = 32
TOP_K = 4
D_EXPERT = D_MODEL
SWIGLU_ALPHA = 1.702
SWIGLU_LIMIT = 7.0
NORM_EPS = 1e-6
N_MOD = 6
FGATE_BIAS_INIT = 3.0
NEG_INF = -1e30
IN_COLS = 3 * D_ATTN + N_HEADS + 3 * D_CONV

kernel_name = 'hymba_fox_shortconv_moe_adaln_step'


def rmsnorm(x, g):
    xf = x.astype(jnp.float32)
    y = xf * lax.rsqrt(jnp.mean(xf * xf, axis=-1, keepdims=True) + NORM_EPS)
    return (y * g.astype(jnp.float32)).astype(x.dtype)


def split_heads(t):
    return t.reshape(t.shape[0], t.shape[1], N_HEADS, HEAD_DIM)


def fox_prompt(q, k, v, logf):
    b, t = q.shape[0], q.shape[1]
    nb = t // Q_BLOCK
    scale = HEAD_DIM ** -0.5
    cum = jnp.transpose(lax.cumsum(logf.astype(jnp.float32), axis=1), (0, 2, 1))
    q_blocks = jnp.moveaxis(q.reshape(b, nb, Q_BLOCK, N_HEADS, HEAD_DIM), 1, 0)
    c_blocks = jnp.moveaxis(cum.reshape(b, N_HEADS, nb, Q_BLOCK), 2, 0)
    k_pos = jnp.arange(t)

    def block(args):
        q_i, c_i, i = args
        s = jnp.einsum('bqhd,bkhd->bhqk', q_i, k, preferred_element_type=jnp.float32) * scale
        s = s + c_i[..., :, None] - cum[:, :, None, :]
        q_pos = i * Q_BLOCK + jnp.arange(Q_BLOCK)
        s = jnp.where(k_pos[None, :] <= q_pos[:, None], s, NEG_INF)
        p = jax.nn.softmax(s, axis=-1).astype(v.dtype)
        return jnp.einsum('bhqk,bkhd->bqhd', p, v)

    o = lax.map(block, (q_blocks, c_blocks, jnp.arange(nb)))
    return jnp.moveaxis(o, 0, 1).reshape(b, t, D_ATTN)


def fox_sample(q, k, v, logf, k_past, v_past, logf_past):
    b, s_len = q.shape[0], q.shape[1]
    p_len = k_past.shape[1]
    scale = HEAD_DIM ** -0.5
    lp = logf_past.astype(jnp.float32)
    suffix = lax.cumsum(lp, axis=1, reverse=True) - lp
    c_new = jnp.transpose(lax.cumsum(logf.astype(jnp.float32), axis=1), (0, 2, 1))
    s_past = jnp.einsum('bthd,bshd->bhts', q, k_past, preferred_element_type=jnp.float32) * scale
    s_past = s_past + c_new[..., :, None] + jnp.transpose(suffix, (0, 2, 1))[..., None, :]
    s_new = jnp.einsum('bthd,bshd->bhts', q, k, preferred_element_type=jnp.float32) * scale
    s_new = s_new + c_new[..., :, None] - c_new[..., None, :]
    causal = jnp.arange(s_len)[None, :] <= jnp.arange(s_len)[:, None]
    s_new = jnp.where(causal, s_new, NEG_INF)
    p = jax.nn.softmax(jnp.concatenate([s_past, s_new], axis=-1), axis=-1).astype(v.dtype)
    o = (jnp.einsum('bhts,bshd->bthd', p[..., :p_len], v_past)
         + jnp.einsum('bhts,bshd->bthd', p[..., p_len:], v))
    return o.reshape(b, s_len, D_ATTN)


def short_conv(u, buf, conv_w):
    t = u.shape[1]
    u_pad = jnp.concatenate([buf.astype(u.dtype), u], axis=1)
    y = sum(conv_w[j] * u_pad[:, j:j + t] for j in range(CONV_W))
    return y, u_pad[:, -(CONV_W - 1):]


def swiglu_clamped(z):
    g, lin = z[..., :D_EXPERT], z[..., D_EXPERT:]
    g = jnp.minimum(g, SWIGLU_LIMIT)
    lin = jnp.clip(lin, -SWIGLU_LIMIT, SWIGLU_LIMIT)
    return g * jax.nn.sigmoid(SWIGLU_ALPHA * g) * (lin + 1.0)


def moe(h, w_router, b_router, w_up, b_up, w_down, b_down):
    shape = h.shape
    hf = h.reshape(-1, D_MODEL)
    logits = (hf @ w_router + b_router).astype(jnp.float32)
    top_v, top_i = lax.top_k(logits, TOP_K)
    gates = jax.nn.softmax(top_v, axis=-1)
    comb = jnp.einsum('nk,nke->ne', gates,
                      jax.nn.one_hot(top_i, N_EXPERTS, dtype=jnp.float32)).astype(h.dtype)
    out = jnp.zeros_like(hf)
    for e in range(N_EXPERTS):
        z = hf @ w_up[e] + b_up[e]
        out = out + comb[:, e:e + 1] * (swiglu_clamped(z) @ w_down[e] + b_down[e])
    return out.reshape(shape)


def decoder_layer(x, c, conv_buf, attend, w_ada, b_ada, g_pre_mix, g_post_mix, w_in, b_fgate,
                  conv_w, g_attn_out, g_conv_out, w_out, g_pre_ffn, g_post_ffn,
                  w_router, b_router, w_up, b_up, w_down, b_down):
    b, t = x.shape[0], x.shape[1]
    mod = (jax.nn.silu(c) @ w_ada + b_ada).reshape(b, N_MOD, 1, D_MODEL)
    shift_a, scale_a, gate_a = mod[:, 0], mod[:, 1], mod[:, 2]
    shift_m, scale_m, gate_m = mod[:, 3], mod[:, 4], mod[:, 5]

    h = rmsnorm(x, g_pre_mix) * (1.0 + scale_a) + shift_a
    z = h @ w_in
    o1, o2, o3 = D_ATTN, 2 * D_ATTN, 3 * D_ATTN
    o4 = o3 + N_HEADS
    q = split_heads(z[..., :o1])
    k = split_heads(z[..., o1:o2])
    v = split_heads(z[..., o2:o3])
    logf = jax.nn.log_sigmoid((z[..., o3:o4] + b_fgate).astype(jnp.float32))
    g_b = z[..., o4:o4 + D_CONV]
    g_c = z[..., o4 + D_CONV:o4 + 2 * D_CONV]
    h_c = z[..., o4 + 2 * D_CONV:]
    attn = attend(q, k, v, logf)
    y_c, new_buf = short_conv(g_c * h_c, conv_buf, conv_w)
    conv_out = g_b * y_c
    merged = jnp.concatenate([rmsnorm(attn, g_attn_out), rmsnorm(conv_out, g_conv_out)], axis=-1)
    x = x + gate_a * rmsnorm(merged @ w_out, g_post_mix)

    h2 = rmsnorm(x, g_pre_ffn) * (1.0 + scale_m) + shift_m
    x = x + gate_m * rmsnorm(moe(h2, w_router, b_router, w_up, b_up, w_down, b_down), g_post_ffn)
    return x, k, v, logf.astype(k.dtype), new_buf


def setup_inputs(seed: int = 0) -> dict:
    key = jax.random.key(seed)
    ks = jax.random.split(key, 32)
    f32 = jnp.float32
    n_pages = PAST_LEN // PAGE_SIZE
    n_used = DEC_BATCH * n_pages
    n_pool = n_used + max(1, n_used // 4)
    nrm = lambda k, s, sc: jax.random.normal(k, s, f32) * sc
    page_table = jax.random.permutation(ks[0], n_pool)[:n_used].reshape(DEC_BATCH, n_pages).astype(jnp.int32)
    return {
        'x_prompt': nrm(ks[1], (BATCH, SEQ, D_MODEL), 1.0),
        'x_sample': nrm(ks[2], (DEC_BATCH, DEC_SEQ, D_MODEL), 1.0),
        'cache_k': nrm(ks[3], (DEPTH, n_pool, PAGE_SIZE, N_HEADS, HEAD_DIM), 1.0),
        'cache_v': nrm(ks[4], (DEPTH, n_pool, PAGE_SIZE, N_HEADS, HEAD_DIM), 1.0),
        'cache_logf': jax.nn.log_sigmoid(FGATE_BIAS_INIT + nrm(ks[5], (DEPTH, n_pool, PAGE_SIZE, N_HEADS), 0.5)),
        'state_conv': nrm(ks[6], (DEPTH, DEC_BATCH, CONV_W - 1, D_CONV), 0.5),
        'page_table': page_table,
        'c_prompt': nrm(ks[7], (BATCH, D_MODEL), 1.0),
        'c_sample': nrm(ks[8], (DEC_BATCH, D_MODEL), 1.0),
        'w_ada': nrm(ks[9], (DEPTH, D_MODEL, N_MOD * D_MODEL), 0.5 * D_MODEL ** -0.5),
        'b_ada': nrm(ks[10], (DEPTH, N_MOD * D_MODEL), 0.01),
        'g_pre_mix': 1.0 + nrm(ks[11], (DEPTH, D_MODEL), 0.05),
        'g_post_mix': 1.0 + nrm(ks[12], (DEPTH, D_MODEL), 0.05),
        'w_in': nrm(ks[13], (DEPTH, D_MODEL, IN_COLS), D_MODEL ** -0.5),
        'b_fgate': FGATE_BIAS_INIT + nrm(ks[14], (DEPTH, N_HEADS), 0.5),
        'conv_w': nrm(ks[15], (DEPTH, CONV_W, D_CONV), CONV_W ** -0.5),
        'g_attn_out': 1.0 + nrm(ks[16], (DEPTH, D_ATTN), 0.05),
        'g_conv_out': 1.0 + nrm(ks[17], (DEPTH, D_CONV), 0.05),
        'w_out': nrm(ks[18], (DEPTH, D_MIX, D_MODEL), D_MIX ** -0.5),
        'g_pre_ffn': 1.0 + nrm(ks[19], (DEPTH, D_MODEL), 0.05),
        'g_post_ffn': 1.0 + nrm(ks[20], (DEPTH, D_MODEL), 0.05),
        'w_router': nrm(ks[21], (DEPTH, D_MODEL, N_EXPERTS), D_MODEL ** -0.5),
        'b_router': nrm(ks[22], (DEPTH, N_EXPERTS), 0.01),
        'w_up': nrm(ks[23], (DEPTH, N_EXPERTS, D_MODEL, 2 * D_EXPERT), D_MODEL ** -0.5),
        'b_up': nrm(ks[24], (DEPTH, N_EXPERTS, 2 * D_EXPERT), 0.01),
        'w_down': nrm(ks[25], (DEPTH, N_EXPERTS, D_EXPERT, D_MODEL), D_EXPERT ** -0.5),
        'b_down': nrm(ks[26], (DEPTH, N_EXPERTS, D_MODEL), 0.01),
    }


def reference(x_prompt, x_sample, cache_k, cache_v, cache_logf, state_conv, page_table,
              c_prompt, c_sample, w_ada, b_ada, g_pre_mix, g_post_mix, w_in, b_fgate, conv_w,
              g_attn_out, g_conv_out, w_out, g_pre_ffn, g_post_ffn, w_router, b_router,
              w_up, b_up, w_down, b_down):
    dec_b, n_pages = page_table.shape
    p_len = n_pages * PAGE_SIZE
    yp, ys = x_prompt, x_sample
    kp_l, vp_l, lfp_l, cp_l = [], [], [], []
    ks_l, vs_l, lfs_l, cs_l = [], [], [], []
    for l in range(DEPTH):
        lw = (w_ada[l], b_ada[l], g_pre_mix[l], g_post_mix[l], w_in[l], b_fgate[l], conv_w[l],
              g_attn_out[l], g_conv_out[l], w_out[l], g_pre_ffn[l], g_post_ffn[l],
              w_router[l], b_router[l], w_up[l], b_up[l], w_down[l], b_down[l])
        zero_buf = jnp.zeros((yp.shape[0], CONV_W - 1, D_CONV), yp.dtype)
        yp, kp, vp, lfp, cbp = decoder_layer(yp, c_prompt, zero_buf, fox_prompt, *lw)
        k_past = cache_k[l][page_table].reshape(dec_b, p_len, N_HEADS, HEAD_DIM)
        v_past = cache_v[l][page_table].reshape(dec_b, p_len, N_HEADS, HEAD_DIM)
        lf_past = cache_logf[l][page_table].reshape(dec_b, p_len, N_HEADS)
        attend_s = functools.partial(fox_sample, k_past=k_past, v_past=v_past, logf_past=lf_past)
        ys, ksn, vsn, lfs, cbs = decoder_layer(ys, c_sample, state_conv[l], attend_s, *lw)
        kp_l.append(kp); vp_l.append(vp); lfp_l.append(lfp); cp_l.append(cbp)
        ks_l.append(ksn); vs_l.append(vsn); lfs_l.append(lfs); cs_l.append(cbs)
    return (yp, ys, jnp.stack(kp_l), jnp.stack(vp_l), jnp.stack(lfp_l), jnp.stack(cp_l),
            jnp.stack(ks_l), jnp.stack(vs_l), jnp.stack(lfs_l), jnp.stack(cs_l))
```

```python
import functools

import jax
import jax.numpy as jnp
from jax import lax
from jax.experimental import pallas as pl
from jax.experimental.pallas import tpu as pltpu

F32 = jnp.float32
BF16 = jnp.bfloat16

HEAD_DIM = 64
TOP_K = 4
CONV_W = 3
NORM_EPS = 1e-6
SWIGLU_ALPHA = 1.702
SWIGLU_LIMIT = 7.0
NEG_INF = -1e30

LANES = 128
VMEM_LIMIT = 56 * 1024 * 1024

SUPER_ROWS = 1024
SUB_ROWS = 256
EXPERT_CHUNK = 256


def _cp(*sem):
    return pltpu.CompilerParams(dimension_semantics=sem, vmem_limit_bytes=VMEM_LIMIT)


def _dot(a, b):
    return jnp.dot(a, b, preferred_element_type=F32)


def _dot_nt(a, b):
    return lax.dot_general(a, b, (((1,), (1,)), ((), ())), preferred_element_type=F32)


def _split3(x):
    hi = x.astype(BF16)
    r1 = x - hi.astype(F32)
    mid = r1.astype(BF16)
    lo = (r1 - mid.astype(F32)).astype(BF16)
    return hi, mid, lo


def _split2(x):
    hi = x.astype(BF16)
    return hi, (x - hi.astype(F32)).astype(BF16)


def _dot3(a, b):
    ah, al = _split2(a)
    bh, bl = _split2(b)
    return _dot(ah, bh) + _dot(al, bh) + _dot(ah, bl)


def _tile(n, pref):
    if n <= pref:
        return n
    t = pref - pref % LANES
    while n % t:
        t -= LANES
    return t


def _rms(x, g):
    return x * lax.rsqrt(jnp.mean(x * x, axis=-1, keepdims=True) + NORM_EPS) * g


def _row_spec(rm, tm, d):
    if rm == 1:
        return pl.BlockSpec((1, d), lambda i: (0, 0))
    return pl.BlockSpec((tm, d), lambda i: (i, 0))


def _ada_kernel(c_ref, w_ref, b_ref, o_ref):
    c = c_ref[...]
    a = c * jax.nn.sigmoid(c)
    o_ref[...] = _dot3(a, w_ref[...]) + b_ref[...]


def _ada_mod(c_all, w_ada, b_ada):
    r, d = c_all.shape
    n = w_ada.shape[1]
    tn = _tile(n, 1024)
    return pl.pallas_call(
        _ada_kernel,
        out_shape=jax.ShapeDtypeStruct((r, n), F32),
        grid=(n // tn,),
        in_specs=[pl.BlockSpec((r, d), lambda j: (0, 0)),
                  pl.BlockSpec((d, tn), lambda j: (0, j)),
                  pl.BlockSpec((1, tn), lambda j: (0, j))],
        out_specs=pl.BlockSpec((r, tn), lambda j: (0, j)),
        compiler_params=_cp("arbitrary"),
        name="ada_mod",
    )(c_all, w_ada, b_ada.reshape(1, n))


def _prenorm_kernel(x_ref, g_ref, sc_ref, sh_ref, wf_ref, h_ref, zf_ref, *, precise):
    h = _rms(x_ref[...], g_ref[...]) * (1.0 + sc_ref[...]) + sh_ref[...]
    if precise:
        h_ref[...] = h
        zf_ref[...] = _dot3(h, wf_ref[...])
    else:
        hb = h.astype(BF16)
        h_ref[...] = hb
        zf_ref[...] = _dot(hb, wf_ref[...])


def _prenorm(x, g, scale, shift, wf_pad, precise):
    r, d = x.shape
    tm = min(256, r)
    rm = scale.shape[0]
    return pl.pallas_call(
        functools.partial(_prenorm_kernel, precise=precise),
        out_shape=(jax.ShapeDtypeStruct((r, d), F32 if precise else BF16), jax.ShapeDtypeStruct((r, LANES), F32)),
        grid=(r // tm,),
        in_specs=[pl.BlockSpec((tm, d), lambda i: (i, 0)),
                  pl.BlockSpec((1, d), lambda i: (0, 0)),
                  _row_spec(rm, tm, d), _row_spec(rm, tm, d),
                  pl.BlockSpec((d, LANES), lambda i: (0, 0))],
        out_specs=(pl.BlockSpec((tm, d), lambda i: (i, 0)),
                   pl.BlockSpec((tm, LANES), lambda i: (i, 0))),
        compiler_params=_cp("arbitrary"),
        name="prenorm",
    )(x, g.reshape(1, d), scale, shift, wf_pad)


def _proj_kernel(h_ref, w_ref, *o_refs, want_f32):
    z = _dot(h_ref[...], w_ref[...])
    if want_f32:
        o_refs[0][...] = z
        o_refs[1][...] = z.astype(BF16)
    else:
        o_refs[0][...] = z.astype(BF16)


def _proj(h, w, want_f32):
    r, d = h.shape
    n = w.shape[1]
    tm = min(512, r)
    shapes, specs = [], []
    if want_f32:
        shapes.append(jax.ShapeDtypeStruct((r, n), F32))
        specs.append(pl.BlockSpec((tm, n), lambda i: (i, 0)))
    shapes.append(jax.ShapeDtypeStruct((r, n), BF16))
    specs.append(pl.BlockSpec((tm, n), lambda i: (i, 0)))
    return pl.pallas_call(
        functools.partial(_proj_kernel, want_f32=want_f32),
        out_shape=tuple(shapes),
        grid=(r // tm,),
        in_specs=[pl.BlockSpec((tm, d), lambda i: (i, 0)),
                  pl.BlockSpec((d, n), lambda i: (0, 0))],
        out_specs=tuple(specs),
        compiler_params=_cp("arbitrary"),
        name="proj",
    )(h, w)


def _proj3_kernel(h_ref, w_ref, o_ref):
    o_ref[...] = _dot3(h_ref[...], w_ref[...])


def _proj_precise(h, w, n):
    r, d = h.shape
    tn = _tile(n, 512)
    return pl.pallas_call(
        _proj3_kernel,
        out_shape=jax.ShapeDtypeStruct((r, n), F32),
        grid=(n // tn,),
        in_specs=[pl.BlockSpec((r, d), lambda j: (0, 0)),
                  pl.BlockSpec((d, tn), lambda j: (0, j))],
        out_specs=pl.BlockSpec((r, tn), lambda j: (0, j)),
        compiler_params=_cp("arbitrary"),
        name="proj_precise",
    )(h, w)


def _conv_kernel(*refs, tm, seq_rows, precise):
    if seq_rows is None:
        h_ref, wb_ref, wc_ref, wh_ref, cw_ref, buf_ref, o_ref, tail_ref, carry = refs
    else:
        h_ref, wb_ref, wc_ref, wh_ref, cw_ref, b1_ref, b2_ref, o_ref, u_ref = refs
    mm = _dot3 if precise else _dot
    h = h_ref[...]
    gb = mm(h, wb_ref[...])
    gc = mm(h, wc_ref[...])
    hc = mm(h, wh_ref[...])
    u = gc * hc
    rows = lax.broadcasted_iota(jnp.int32, u.shape, 0)
    r1 = pltpu.roll(u, 1, 0)
    r2 = pltpu.roll(u, 2, 0)
    if seq_rows is None:
        @pl.when(pl.program_id(1) == 0)
        def _():
            carry[...] = buf_ref[...]
        c0 = carry[0:1, :]
        c1 = carry[1:2, :]
        u1 = jnp.where(rows >= 1, r1, c1)
        u2 = jnp.where(rows >= 2, r2, jnp.where(rows == 0, c0, c1))
        carry[...] = u[tm - 2:tm, :]
        tail_ref[...] = u[tm - 2:tm, :]
    else:
        pos = rows % seq_rows
        u1 = jnp.where(pos >= 1, r1, b1_ref[...])
        u2 = jnp.where(pos >= 2, r2, b2_ref[...])
        u_ref[...] = u
    cw = cw_ref[...]
    y = cw[0:1, :] * u2 + cw[1:2, :] * u1 + cw[2:3, :] * u
    o_ref[...] = gb * y


def _conv_group(h, w_c, conv_w, hist, seq_rows, precise):
    r, d = h.shape
    c = w_c.shape[1] // 3
    tn = _tile(c, 256 if precise else 512)
    nb = c // tn
    tm = min(512, r)
    in_specs = [pl.BlockSpec((tm, d), lambda j, i: (i, 0)),
                pl.BlockSpec((d, tn), lambda j, i: (0, j)),
                pl.BlockSpec((d, tn), lambda j, i: (0, nb + j)),
                pl.BlockSpec((d, tn), lambda j, i: (0, 2 * nb + j)),
                pl.BlockSpec((CONV_W, tn), lambda j, i: (0, j))]
    if seq_rows is None:
        in_specs.append(pl.BlockSpec((2, tn), lambda j, i: (0, j)))
        args = (hist,)
        out_shape = (jax.ShapeDtypeStruct((r, c), F32), jax.ShapeDtypeStruct((2, c), F32))
        out_specs = (pl.BlockSpec((tm, tn), lambda j, i: (i, j)), pl.BlockSpec((2, tn), lambda j, i: (0, j)))
        scratch = [pltpu.VMEM((2, tn), F32)]
    else:
        assert tm == r and r % seq_rows == 0
        in_specs += [pl.BlockSpec((tm, tn), lambda j, i: (i, j))] * 2
        args = hist
        out_shape = (jax.ShapeDtypeStruct((r, c), F32), jax.ShapeDtypeStruct((r, c), F32))
        out_specs = (pl.BlockSpec((tm, tn), lambda j, i: (i, j)),) * 2
        scratch = []
    return pl.pallas_call(
        functools.partial(_conv_kernel, tm=tm, seq_rows=seq_rows, precise=precise),
        out_shape=out_shape,
        grid=(nb, r // tm),
        in_specs=in_specs,
        out_specs=out_specs,
        scratch_shapes=scratch,
        compiler_params=_cp("arbitrary", "arbitrary"),
        name="conv_group",
    )(h, w_c, w_c, w_c, conv_w, *args)


def _fgate_kernel(zf_ref, b_ref, tri_ref, lf_ref, c_ref, carry, *, tm, chained):
    x = zf_ref[...] + b_ref[...]
    lf = jnp.minimum(x, 0.0) - jnp.log1p(jnp.exp(-jnp.abs(x)))
    lf_ref[...] = lf
    tri = tri_ref[...]
    hi, mid, lo = _split3(lf)
    c = _dot(tri, hi) + _dot(tri, mid) + _dot(tri, lo)
    if chained:
        @pl.when(pl.program_id(0) == 0)
        def _():
            carry[...] = jnp.zeros_like(carry)
        c = c + carry[...]
        carry[...] = c[tm - 1:tm, :]
    c_ref[...] = c


def _fgate(zf, b_pad, tri, chained):
    r = zf.shape[0]
    tm = tri.shape[0]
    return pl.pallas_call(
        functools.partial(_fgate_kernel, tm=tm, chained=chained),
        out_shape=(jax.ShapeDtypeStruct((r, LANES), F32), jax.ShapeDtypeStruct((r, LANES), F32)),
        grid=(r // tm,),
        in_specs=[pl.BlockSpec((tm, LANES), lambda i: (i, 0)),
                  pl.BlockSpec((1, LANES), lambda i: (0, 0)),
                  pl.BlockSpec((tm, tm), lambda i: (0, 0))],
        out_specs=(pl.BlockSpec((tm, LANES), lambda i: (i, 0)),) * 2,
        scratch_shapes=[pltpu.VMEM((1, LANES), F32)],
        compiler_params=_cp("arbitrary"),
        name="fgate",
    )(zf, b_pad, tri)


def _attn_kernel(q_ref, k_ref, v_ref, cc_ref, cr_ref, o_ref, m_ref, l_ref, acc_ref, *, tq):
    qi = pl.program_id(1)
    ki = pl.program_id(2)

    @pl.when(ki == 0)
    def _():
        m_ref[...] = jnp.full_like(m_ref, NEG_INF)
        l_ref[...] = jnp.zeros_like(l_ref)
        acc_ref[...] = jnp.zeros_like(acc_ref)

    def step(masked):
        q = q_ref[...]
        k = k_ref[...]
        v = v_ref[...]
        cc = cc_ref[...]
        cr = cr_ref[...]
        lane = lax.broadcasted_iota(jnp.int32, q.shape, 1)
        lower = lane < HEAD_DIM
        alphas, pvs = [], []
        for h in range(2):
            sel = lower if h == 0 else jnp.logical_not(lower)
            qh = jnp.where(sel, q, jnp.zeros_like(q)) * jnp.asarray(HEAD_DIM ** -0.5, BF16)
            s = _dot_nt(qh, k) + cc[:, h:h + 1] - cr[h:h + 1, :]
            if masked:
                row = lax.broadcasted_iota(jnp.int32, s.shape, 0)
                col = lax.broadcasted_iota(jnp.int32, s.shape, 1)
                s = jnp.where(col <= row, s, NEG_INF)
            m_prev = m_ref[h]
            m_new = jnp.maximum(m_prev, jnp.max(s, axis=-1, keepdims=True))
            alpha = jnp.exp(m_prev - m_new)
            p = jnp.exp(s - m_new)
            l_ref[h] = alpha * l_ref[h] + jnp.sum(p, axis=-1, keepdims=True)
            m_ref[h] = m_new
            alphas.append(alpha)
            pvs.append(_dot(p.astype(BF16), v))
        acc_ref[...] = acc_ref[...] * jnp.where(lower, alphas[0], alphas[1]) + jnp.where(lower, pvs[0], pvs[1])

    @pl.when(ki < qi)
    def _():
        step(False)

    @pl.when(ki == qi)
    def _():
        step(True)
        lane = lax.broadcasted_iota(jnp.int32, acc_ref.shape, 1)
        o_ref[...] = acc_ref[...] / jnp.where(lane < HEAD_DIM, l_ref[0], l_ref[1])


def _attn_prompt(qb, kb, vb, ccol, crow):
    t, da = qb.shape
    npair = da // LANES
    tq = min(512, t)
    nq = t // tq
    kv_map = lambda hp, qi, ki: (jnp.minimum(ki, qi), hp)
    return pl.pallas_call(
        functools.partial(_attn_kernel, tq=tq),
        out_shape=jax.ShapeDtypeStruct((t, da), F32),
        grid=(npair, nq, nq),
        in_specs=[pl.BlockSpec((tq, LANES), lambda hp, qi, ki: (qi, hp)),
                  pl.BlockSpec((tq, LANES), kv_map),
                  pl.BlockSpec((tq, LANES), kv_map),
                  pl.BlockSpec((None, tq, 2), lambda hp, qi, ki: (hp, qi, 0)),
                  pl.BlockSpec((None, 2, tq), lambda hp, qi, ki: (hp, 0, jnp.minimum(ki, qi)))],
        out_specs=pl.BlockSpec((tq, LANES), lambda hp, qi, ki: (qi, hp)),
        scratch_shapes=[pltpu.VMEM((2, tq, 1), F32), pltpu.VMEM((2, tq, 1), F32),
                        pltpu.VMEM((tq, LANES), F32)],
        compiler_params=_cp("arbitrary", "arbitrary", "arbitrary"),
        name="attn_prompt",
    )(qb, kb, vb, ccol, crow)


def _pool_kernel(lf_ref, m_ref, ones_ref, w_ref, s_ref):
    hi, mid, lo = _split3(lf_ref[...])
    m = m_ref[...]
    o = ones_ref[...]
    w_ref[...] = _dot(hi, m) + _dot(mid, m) + _dot(lo, m)
    s_ref[...] = _dot(hi, o) + _dot(mid, o) + _dot(lo, o)


def _pool_suffix(lf_t):
    r, p = lf_t.shape
    tm = 2048 if r % 2048 == 0 else r
    i_idx = lax.broadcasted_iota(jnp.int32, (p, p), 0)
    j_idx = lax.broadcasted_iota(jnp.int32, (p, p), 1)
    later = (i_idx > j_idx).astype(BF16)
    ones = jnp.ones((p, p), BF16)
    return pl.pallas_call(
        _pool_kernel,
        out_shape=(jax.ShapeDtypeStruct((r, p), F32),) * 2,
        grid=(r // tm,),
        in_specs=[pl.BlockSpec((tm, p), lambda i: (i, 0)),
                  pl.BlockSpec((p, p), lambda i: (0, 0)),
                  pl.BlockSpec((p, p), lambda i: (0, 0))],
        out_specs=(pl.BlockSpec((tm, p), lambda i: (i, 0)),) * 2,
        compiler_params=_cp("arbitrary"),
        name="pool_suffix",
    )(lf_t, later, ones)


def _attn_sample_kernel(pt_ref, q_ref, kn_ref, vn_ref, cq_ref, ck_ref, hm_ref, *rest, pages_per_step, n_new, n_heads):
    npg = pages_per_step
    k_refs = rest[0:npg]
    v_refs = rest[npg:2 * npg]
    w_refs = rest[2 * npg:3 * npg]
    s_refs = rest[3 * npg:4 * npg]
    o_ref, m_ref, l_ref, acc_ref, carry_ref = rest[4 * npg:]
    g = pl.program_id(1)
    rows = n_new * n_heads
    cq = cq_ref[...]
    qh, ql = _split2(q_ref[...])
    q2 = jnp.concatenate([qh, ql], axis=0)

    def scores(k32):
        kh, kl = _split2(k32)
        s2 = _dot_nt(q2, kh)
        return s2[:rows] + s2[rows:] + _dot_nt(qh, kl)

    def update(s, v32):
        m_prev = m_ref[...]
        m_new = jnp.maximum(m_prev, jnp.max(s, axis=-1, keepdims=True))
        alpha = jnp.exp(m_prev - m_new)
        p = jnp.exp(s - m_new)
        l_ref[...] = alpha * l_ref[...] + jnp.sum(p, axis=-1, keepdims=True)
        m_ref[...] = m_new
        ph, pl_ = _split2(p)
        vh, vl = _split2(v32)
        pv2 = _dot(jnp.concatenate([ph, pl_], axis=0), vh)
        acc_ref[...] = acc_ref[...] * alpha + (pv2[:rows] + pv2[rows:] + _dot(ph, vl))

    @pl.when(g == 0)
    def _():
        m_ref[...] = jnp.full_like(m_ref, NEG_INF)
        l_ref[...] = jnp.zeros_like(l_ref)
        acc_ref[...] = jnp.zeros_like(acc_ref)
        carry_ref[...] = jnp.zeros_like(carry_ref)
        s = scores(kn_ref[...]) + cq - ck_ref[...]
        row = lax.broadcasted_iota(jnp.int32, s.shape, 0)
        col = lax.broadcasted_iota(jnp.int32, s.shape, 1)
        s = jnp.where(col <= row // n_heads, s, NEG_INF)
        update(s, vn_ref[...])

    for i in range(npg):
        suffix = w_refs[i][...] + carry_ref[...]
        s = scores(k_refs[i][...]) + cq + jnp.concatenate([suffix] * n_new, axis=0)
        carry_ref[...] = carry_ref[...] + s_refs[i][...]
        update(s, v_refs[i][...])

    @pl.when(g == pl.num_programs(1) - 1)
    def _():
        o = acc_ref[...] / l_ref[...]
        o = o.reshape(n_new, n_heads, o.shape[-1]) * hm_ref[...]
        o_ref[...] = jnp.sum(o, axis=1)


def _attn_sample(page_table, q_bd, k_new, v_new, cq, ck, cache_k, cache_v, w_t, s_t, n_new, n_heads):
    nb, n_pages = page_table.shape
    page = cache_k.shape[1]
    da = cache_k.shape[2]
    rows = n_new * n_heads
    npg = 4 if n_pages % 4 == 0 else 1
    steps = n_pages // npg
    hm = (lax.broadcasted_iota(jnp.int32, (n_heads, da), 1) // HEAD_DIM
          == lax.broadcasted_iota(jnp.int32, (n_heads, da), 0)).astype(F32)

    def page_map(i):
        return lambda b, g, pt: (pt[b, n_pages - 1 - (g * npg + i)], 0, 0)

    per_b = lambda b, g, pt: (b, 0, 0)
    in_specs = [pl.BlockSpec((None, rows, da), per_b),
                pl.BlockSpec((None, page, da), per_b),
                pl.BlockSpec((None, page, da), per_b),
                pl.BlockSpec((None, rows, page), per_b),
                pl.BlockSpec((None, rows, page), per_b),
                pl.BlockSpec((n_heads, da), lambda b, g, pt: (0, 0))]
    in_specs += [pl.BlockSpec((None, page, da), page_map(i)) for i in range(npg)]
    in_specs += [pl.BlockSpec((None, page, da), page_map(i)) for i in range(npg)]
    in_specs += [pl.BlockSpec((None, n_heads, page), page_map(i)) for i in range(npg)]
    in_specs += [pl.BlockSpec((None, n_heads, page), page_map(i)) for i in range(npg)]
    grid_spec = pltpu.PrefetchScalarGridSpec(
        num_scalar_prefetch=1,
        grid=(nb, steps),
        in_specs=in_specs,
        out_specs=pl.BlockSpec((None, n_new, da), per_b),
        scratch_shapes=[pltpu.VMEM((rows, 1), F32), pltpu.VMEM((rows, 1), F32),
                        pltpu.VMEM((rows, da), F32), pltpu.VMEM((n_heads, page), F32)],
    )
    return pl.pallas_call(
        functools.partial(_attn_sample_kernel, pages_per_step=npg, n_new=n_new, n_heads=n_heads),
        out_shape=jax.ShapeDtypeStruct((nb, n_new, da), F32),
        grid_spec=grid_spec,
        compiler_params=_cp("arbitrary", "arbitrary"),
        name="attn_sample",
    )(page_table, q_bd, k_new, v_new, cq, ck, hm,
      *([cache_k] * npg), *([cache_v] * npg), *([w_t] * npg), *([s_t] * npg))


def _merge_kernel(attn_ref, conv_ref, gao_ref, gco_ref, o_ref, *, da):
    o_ref[:, :da] = _rms(attn_ref[...], gao_ref[...])
    o_ref[:, da:] = _rms(conv_ref[...], gco_ref[...])


def _merge_norm(attn, conv, g_ao, g_co):
    r, da = attn.shape
    dc = conv.shape[1]
    return pl.pallas_call(
        functools.partial(_merge_kernel, da=da),
        out_shape=jax.ShapeDtypeStruct((r, da + dc), F32),
        name="merge_norm",
    )(attn, conv, g_ao.reshape(1, da), g_co.reshape(1, dc))


def _outproj_kernel(*refs, half, da, fused):
    if fused:
        (attn_ref, conv_ref, gao_ref, gco_ref, wo_ref, x_ref, ga_ref, gpm_ref, gpf_ref, scm_ref, shm_ref,
         wr_ref, br_ref, tri_ref, cin_ref, x1_ref, hw_ref, idx_ref, gate_ref, pos_ref, cnt_ref, carry) = refs
    else:
        (y_ref, x_ref, ga_ref, gpm_ref, gpf_ref, scm_ref, shm_ref,
         wr_ref, br_ref, tri_ref, cin_ref, x1_ref, hw_ref, idx_ref, gate_ref, pos_ref, cnt_ref, carry) = refs

    @pl.when(pl.program_id(0) == 0)
    def _():
        carry[...] = cin_ref[...]

    if fused:
        a = _rms(attn_ref[...], gao_ref[...]).astype(BF16)
        c = _rms(conv_ref[...], gco_ref[...]).astype(BF16)
        y = _dot(a, wo_ref[0:da, :]) + _dot(c, wo_ref[da:, :])
    else:
        y = y_ref[...]
    x1 = x_ref[...] + ga_ref[...] * _rms(y, gpm_ref[...])
    x1_ref[...] = x1
    h2 = _rms(x1, gpf_ref[...]) * (1.0 + scm_ref[...]) + shm_ref[...]
    hb = h2.astype(BF16)
    bits = lax.bitcast_convert_type(hb.astype(F32), jnp.uint32)
    hw_ref[...] = (bits[:, :half] >> 16) | (bits[:, half:] & jnp.uint32(0xFFFF0000))

    logits = (_dot(hb, wr_ref[...]) if fused else _dot3(h2, wr_ref[...])) + br_ref[...]
    lanes = lax.broadcasted_iota(jnp.int32, logits.shape, 1)
    lanes_f = lanes.astype(F32)
    work = logits
    vals, hots = [], []
    idx_out = jnp.zeros(logits.shape, F32)
    for k in range(TOP_K):
        mx = jnp.max(work, axis=-1, keepdims=True)
        ix = jnp.min(jnp.where(work == mx, lanes_f, float(LANES)), axis=-1, keepdims=True)
        hot = lanes_f == ix
        work = jnp.where(hot, -jnp.inf, work)
        vals.append(mx)
        hots.append(hot)
        idx_out = jnp.where(lanes == k, ix, idx_out)
    es = [jnp.exp(v - vals[0]) for v in vals]
    denom = es[0] + es[1] + es[2] + es[3]
    gate_out = jnp.zeros(logits.shape, F32)
    cnt = jnp.zeros(logits.shape, F32)
    for k in range(TOP_K):
        gate_out = jnp.where(lanes == k, es[k] / denom, gate_out)
        cnt = cnt + hots[k].astype(F32)
    rank = _dot(tri_ref[...], cnt.astype(BF16)) + carry[...]
    pos_out = jnp.zeros(logits.shape, F32)
    for k in range(TOP_K):
        pk = jnp.sum(jnp.where(hots[k], rank, 0.0), axis=-1, keepdims=True)
        pos_out = jnp.where(lanes == k, pk, pos_out)
    carry[...] = carry[...] + jnp.sum(cnt, axis=0, keepdims=True)
    idx_ref[...] = idx_out.astype(jnp.int32)
    gate_ref[...] = gate_out
    pos_ref[...] = pos_out.astype(jnp.int32)
    cnt_ref[...] = carry[...]


def _outproj(mix, x, gate_a, g_pm, g_pf, scale_m, shift_m, wr_pad, br_pad, count_in):
    r, d = x.shape
    fused = isinstance(mix, tuple)
    tm = min(256, r)
    rm = gate_a.shape[0]
    half = d // 2
    ii = lax.broadcasted_iota(jnp.int32, (tm, tm), 0)
    jj = lax.broadcasted_iota(jnp.int32, (tm, tm), 1)
    tri = (jj < ii).astype(BF16)
    const = lambda shape: pl.BlockSpec(shape, lambda i: (0, 0))
    rowblk = lambda n: pl.BlockSpec((tm, n), lambda i: (i, 0))
    if fused:
        attn, conv, g_ao, g_co, wo_b = mix
        da, dc = attn.shape[1], conv.shape[1]
        mix_args = (attn, conv, g_ao.reshape(1, da), g_co.reshape(1, dc), wo_b)
        mix_specs = [rowblk(da), rowblk(dc), const((1, da)), const((1, dc)), const((da + dc, d))]
    else:
        da = 0
        mix_args = (mix,)
        mix_specs = [rowblk(d)]
    return pl.pallas_call(
        functools.partial(_outproj_kernel, half=half, da=da, fused=fused),
        out_shape=(jax.ShapeDtypeStruct((r, d), F32),
                   jax.ShapeDtypeStruct((r, half), jnp.uint32),
                   jax.ShapeDtypeStruct((r, LANES), jnp.int32),
                   jax.ShapeDtypeStruct((r, LANES), F32),
                   jax.ShapeDtypeStruct((r, LANES), jnp.int32),
                   jax.ShapeDtypeStruct((1, LANES), F32)),
        grid=(r // tm,),
        in_specs=mix_specs + [rowblk(d), _row_spec(rm, tm, d), const((1, d)), const((1, d)),
                              _row_spec(rm, tm, d), _row_spec(rm, tm, d),
                              const((d, LANES)), const((1, LANES)), const((tm, tm)), const((1, LANES))],
        out_specs=(rowblk(d), rowblk(half), rowblk(LANES), rowblk(LANES), rowblk(LANES), const((1, LANES))),
        scratch_shapes=[pltpu.VMEM((1, LANES), F32)],
        compiler_params=_cp("arbitrary"),
        name="outproj_router",
    )(*mix_args, x, gate_a, g_pm.reshape(1, d), g_pf.reshape(1, d), scale_m, shift_m, wr_pad, br_pad, tri, count_in)


def _row_copy(src, src_row, dst, dst_row, sem):
    return pltpu.make_async_copy(src.at[pl.ds(src_row, 1), :], dst.at[pl.ds(dst_row, 1), :], sem)


def _dispatch_kernel(dest_ref, hw_ref, xs_ref, sem, *, tb):
    base = pl.program_id(0) * tb

    def issue(j, _):
        for k in range(TOP_K):
            _row_copy(hw_ref, base + j, xs_ref, dest_ref[0, 0, k * tb + j], sem).start()
        return 0

    lax.fori_loop(0, tb, issue, 0)

    def drain(j, _):
        for k in range(TOP_K):
            _row_copy(hw_ref, 0, xs_ref, 0, sem).wait()
        return 0

    lax.fori_loop(0, tb, drain, 0)


def _dispatch(hw, dest3, m_pad, tb):
    n, half = hw.shape
    return pl.pallas_call(
        functools.partial(_dispatch_kernel, tb=tb),
        out_shape=jax.ShapeDtypeStruct((m_pad, half), jnp.uint32),
        grid=(n // tb,),
        in_specs=[pl.BlockSpec((1, 1, TOP_K * tb), lambda i: (i, 0, 0), memory_space=pltpu.SMEM),
                  pl.BlockSpec(memory_space=pl.ANY)],
        out_specs=pl.BlockSpec(memory_space=pl.ANY),
        scratch_shapes=[pltpu.SemaphoreType.DMA(())],
        compiler_params=_cp("arbitrary"),
        name="dispatch",
    )(dest3, hw)


def _expert_kernel(exp_ref, blk_ref, nsub_ref, x_ref, wg_ref, wl_ref, bg_ref, bl_ref, wd_ref, bd_ref,
                   o_ref, xb_ref, *, half):
    s = pl.program_id(0)
    c = pl.program_id(1)
    nsub = nsub_ref[s]

    @pl.when(nsub > 0)
    def _():
        @pl.when(c == 0)
        def _():
            w = x_ref[...]
            xb_ref[:, :half] = lax.bitcast_convert_type(w << 16, F32).astype(BF16)
            xb_ref[:, half:] = lax.bitcast_convert_type(w & jnp.uint32(0xFFFF0000), F32).astype(BF16)

        wg = wg_ref[...].astype(BF16)
        wl = wl_ref[...].astype(BF16)
        wd = wd_ref[...].astype(BF16)
        bg = bg_ref[...]
        bl = bl_ref[...]
        for sub in range(SUPER_ROWS // SUB_ROWS):
            rows = slice(sub * SUB_ROWS, (sub + 1) * SUB_ROWS)

            @pl.when(sub < nsub)
            def _():
                x = xb_ref[rows, :]
                gte = jnp.minimum(_dot(x, wg) + bg, SWIGLU_LIMIT)
                lin = jnp.clip(_dot(x, wl) + bl, -SWIGLU_LIMIT, SWIGLU_LIMIT)
                act = gte * jax.nn.sigmoid(SWIGLU_ALPHA * gte) * (lin + 1.0)
                contrib = _dot(act.astype(BF16), wd)

                @pl.when(c == 0)
                def _():
                    o_ref[rows, :] = contrib + bd_ref[...]

                @pl.when(c > 0)
                def _():
                    o_ref[rows, :] = o_ref[rows, :] + contrib


def _experts(xs, sup_exp, sup_blk, sup_nsub, w_up, b_up, w_down, b_down):
    m_pad, half = xs.shape
    n_exp, d, de2 = w_up.shape
    de = de2 // 2
    th = min(EXPERT_CHUNK, de)
    nch = de // th
    nsup = m_pad // SUPER_ROWS
    grid_spec = pltpu.PrefetchScalarGridSpec(
        num_scalar_prefetch=3,
        grid=(nsup, nch),
        in_specs=[pl.BlockSpec((SUPER_ROWS, half), lambda s, c, e, b, n: (b[s], 0)),
                  pl.BlockSpec((None, d, th), lambda s, c, e, b, n: (e[s], 0, c)),
                  pl.BlockSpec((None, d, th), lambda s, c, e, b, n: (e[s], 0, nch + c)),
                  pl.BlockSpec((None, 1, th), lambda s, c, e, b, n: (e[s], 0, c)),
                  pl.BlockSpec((None, 1, th), lambda s, c, e, b, n: (e[s], 0, nch + c)),
                  pl.BlockSpec((None, th, d), lambda s, c, e, b, n: (e[s], c, 0)),
                  pl.BlockSpec((None, 1, d), lambda s, c, e, b, n: (e[s], 0, 0))],
        out_specs=pl.BlockSpec((SUPER_ROWS, d), lambda s, c, e, b, n: (b[s], 0)),
        scratch_shapes=[pltpu.VMEM((SUPER_ROWS, d), BF16)],
    )
    return pl.pallas_call(
        functools.partial(_expert_kernel, half=half),
        out_shape=jax.ShapeDtypeStruct((m_pad, d), F32),
        grid_spec=grid_spec,
        compiler_params=_cp("arbitrary", "arbitrary"),
        name="experts",
    )(sup_exp, sup_blk, sup_nsub, xs, w_up, w_up, b_up.reshape(n_exp, 1, de2), b_up.reshape(n_exp, 1, de2),
      w_down, b_down.reshape(n_exp, 1, d))


def _combine_kernel(dest_ref, gate_ref, x1_ref, gm_ref, gpf_ref, ys_ref, y_ref, buf, sem, *, tb):
    def issue(j, _):
        for k in range(TOP_K):
            pltpu.make_async_copy(ys_ref.at[pl.ds(dest_ref[0, 0, k * tb + j], 1), :],
                                  buf.at[k, pl.ds(j, 1), :], sem).start()
        return 0

    lax.fori_loop(0, tb, issue, 0)

    def drain(j, _):
        for k in range(TOP_K):
            pltpu.make_async_copy(ys_ref.at[pl.ds(0, 1), :], buf.at[0, pl.ds(0, 1), :], sem).wait()
        return 0

    lax.fori_loop(0, tb, drain, 0)
    gates = gate_ref[...]
    moe = gates[:, 0:1] * buf[0]
    for k in range(1, TOP_K):
        moe = moe + gates[:, k:k + 1] * buf[k]
    y_ref[...] = x1_ref[...] + gm_ref[...] * _rms(moe, gpf_ref[...])


def _combine(ys, dest3, gates, x1, gate_m, g_post, tb):
    n, d = x1.shape
    rm = gate_m.shape[0]
    return pl.pallas_call(
        functools.partial(_combine_kernel, tb=tb),
        out_shape=jax.ShapeDtypeStruct((n, d), F32),
        grid=(n // tb,),
        in_specs=[pl.BlockSpec((1, 1, TOP_K * tb), lambda i: (i, 0, 0), memory_space=pltpu.SMEM),
                  pl.BlockSpec((tb, LANES), lambda i: (i, 0)),
                  pl.BlockSpec((tb, d), lambda i: (i, 0)),
                  _row_spec(rm, tb, d),
                  pl.BlockSpec((1, d), lambda i: (0, 0)),
                  pl.BlockSpec(memory_space=pl.ANY)],
        out_specs=pl.BlockSpec((tb, d), lambda i: (i, 0)),
        scratch_shapes=[pltpu.VMEM((TOP_K, tb, d), F32), pltpu.SemaphoreType.DMA(())],
        compiler_params=_cp("arbitrary"),
        name="combine",
    )(dest3, gates, x1, gate_m, g_post.reshape(1, d), ys)


def _dest_blocks(dest, tb):
    n, k = dest.shape
    return dest.reshape(n // tb, tb, k).transpose(0, 2, 1).reshape(n // tb, 1, k * tb)


def _pad_lanes(a, fill=0.0):
    return jnp.pad(a, ((0, 0), (0, LANES - a.shape[1])), constant_values=fill)


def kernel(x_prompt, x_sample, cache_k, cache_v, cache_logf, state_conv, page_table, c_prompt, c_sample,
           w_ada, b_ada, g_pre_mix, g_post_mix, w_in, b_fgate, conv_w, g_attn_out, g_conv_out, w_out,
           g_pre_ffn, g_post_ffn, w_router, b_router, w_up, b_up, w_down, b_down):
    depth = w_ada.shape[0]
    bp, t, d = x_prompt.shape
    bs, s_new, _ = x_sample.shape
    n_pool, page, n_heads, hd = cache_k.shape[1:]
    assert hd == HEAD_DIM and bp == 1 and s_new >= CONV_W - 1
    da = n_heads * hd
    dc = state_conv.shape[-1]
    n_exp = w_router.shape[-1]
    n_pages = page_table.shape[1]
    rs = bs * s_new
    o3 = 3 * da
    o4 = o3 + n_heads

    yp = x_prompt.reshape(t, d)
    ys = x_sample.reshape(rs, d)
    outs = {k: [] for k in ("kp", "vp", "lfp", "cp", "ks", "vs", "lfs", "cs")}

    for l in range(depth):
        c_all = jnp.concatenate([c_prompt, c_sample], axis=0)
        pad_r = (-c_all.shape[0]) % 8
        c_all = jnp.pad(c_all, ((0, pad_r), (0, 0)))
        mod = _ada_mod(c_all, w_ada[l], b_ada[l])
        mod_p = [mod[0:bp, i * d:(i + 1) * d] for i in range(6)]
        mod_s = [jnp.repeat(mod[bp:bp + bs, i * d:(i + 1) * d], s_new, axis=0) for i in range(6)]

        w_in_l = w_in[l]
        w_in_b = w_in_l.astype(BF16)
        wf32 = _pad_lanes(w_in_l[:, o3:o4])
        wc32 = w_in_l[:, o4:]
        bf_pad = _pad_lanes(b_fgate[l].reshape(1, n_heads))
        wo_b = w_out[l].astype(BF16)
        wr32 = _pad_lanes(w_router[l])
        br_pad = _pad_lanes(b_router[l].reshape(1, n_exp), fill=NEG_INF)

        hp, zf = _prenorm(yp, g_pre_mix[l], mod_p[1], mod_p[0], wf32.astype(BF16), False)
        qb, = _proj(hp, w_in_b[:, 0:da], False)
        k32, kb = _proj(hp, w_in_b[:, da:2 * da], True)
        v32, vb = _proj(hp, w_in_b[:, 2 * da:o3], True)
        conv_p, tail_p = _conv_group(hp, w_in_b[:, o4:], conv_w[l], jnp.zeros((CONV_W - 1, dc), F32), None, False)
        tmc = min(256, t)
        tri_p = jnp.tril(jnp.ones((tmc, tmc), BF16))
        lf_p, cum_p = _fgate(zf, bf_pad, tri_p, True)
        cum_h = cum_p[:, :n_heads]
        ccol = cum_h.reshape(t, n_heads // 2, 2).transpose(1, 0, 2)
        crow = cum_h.T.reshape(n_heads // 2, 2, t)
        attn_p = _attn_prompt(qb, kb, vb, ccol, crow)

        st = state_conv[l]
        posn = jnp.arange(rs) % s_new
        st_new = jnp.repeat(st[:, 1], s_new, axis=0)
        st_old = jnp.repeat(st[:, 0], s_new, axis=0)
        b1 = jnp.where((posn == 0)[:, None], st_new, 0.0)
        b2 = jnp.where((posn == 0)[:, None], st_old, jnp.where((posn == 1)[:, None], st_new, 0.0))
        hs, zfs = _prenorm(ys, g_pre_mix[l], mod_s[1], mod_s[0], wf32, True)
        qkv_s = _proj_precise(hs, w_in_l, o3)
        qs32, ks32, vs32 = qkv_s[:, 0:da], qkv_s[:, da:2 * da], qkv_s[:, 2 * da:o3]
        conv_s, u_s = _conv_group(hs, wc32, conv_w[l], (b1, b2), s_new, True)
        tri_s = jnp.kron(jnp.eye(bs, dtype=F32), jnp.tril(jnp.ones((s_new, s_new), F32))).astype(BF16)
        lf_s, cum_s = _fgate(zfs, bf_pad, tri_s, False)
        cn = cum_s[:, :n_heads].reshape(bs, s_new, n_heads)
        rows_a = s_new * n_heads
        cq = jnp.broadcast_to(cn.reshape(bs, rows_a, 1), (bs, rows_a, page))
        ck = jnp.broadcast_to(cn.transpose(0, 2, 1)[:, None], (bs, s_new, n_heads, s_new)).reshape(bs, rows_a, s_new)
        ck = jnp.pad(ck, ((0, 0), (0, 0), (0, page - s_new)))
        q4 = qs32.reshape(bs, s_new, n_heads, hd) * (hd ** -0.5)
        eye_h = jnp.eye(n_heads, dtype=F32)
        q_bd = (q4[:, :, :, None, :] * eye_h[None, None, :, :, None]).reshape(bs, rows_a, da)
        k_new = jnp.pad(ks32.reshape(bs, s_new, da), ((0, 0), (0, page - s_new), (0, 0)))
        v_new = jnp.pad(vs32.reshape(bs, s_new, da), ((0, 0), (0, page - s_new), (0, 0)))
        lf_t = cache_logf[l].transpose(0, 2, 1).reshape(n_pool * n_heads, page)
        w_t, s_t = _pool_suffix(lf_t)
        attn_s = _attn_sample(page_table, q_bd, k_new, v_new, cq, ck,
                              cache_k[l].reshape(n_pool, page, da), cache_v[l].reshape(n_pool, page, da),
                              w_t.reshape(n_pool, n_heads, page), s_t.reshape(n_pool, n_heads, page),
                              s_new, n_heads).reshape(rs, da)

        zero_cnt = jnp.zeros((1, LANES), F32)
        x1p, hwp, idxp, gatep, posp, cntp = _outproj(
            (attn_p, conv_p, g_attn_out[l], g_conv_out[l], wo_b), yp, mod_p[2], g_post_mix[l],
            g_pre_ffn[l], mod_p[4], mod_p[3], wr32.astype(BF16), br_pad, zero_cnt)
        y_s = _proj_precise(_merge_norm(attn_s, conv_s, g_attn_out[l], g_conv_out[l]), w_out[l], d)
        x1s, hws, idxs, gates_s, poss, cnt_all = _outproj(
            y_s, ys, mod_s[2], g_post_mix[l], g_pre_ffn[l], mod_s[4], mod_s[3], wr32, br_pad, cntp)

        counts = cnt_all[0, :n_exp].astype(jnp.int32)
        nsup_e = (counts + SUPER_ROWS - 1) // SUPER_ROWS
        sup_end = jnp.cumsum(nsup_e)
        sup_start = sup_end - nsup_e
        n_assign = (t + rs) * TOP_K
        n_sup = (n_assign + n_exp * (SUPER_ROWS - 1)) // SUPER_ROWS
        m_pad = n_sup * SUPER_ROWS
        s_ids = jnp.minimum(jnp.arange(n_sup, dtype=jnp.int32), sup_end[-1] - 1)
        sup_exp = jnp.minimum(jnp.searchsorted(sup_end, s_ids, side="right"), n_exp - 1).astype(jnp.int32)
        rows_left = counts[sup_exp] - (s_ids - sup_start[sup_exp]) * SUPER_ROWS
        sup_nsub = (jnp.clip(rows_left, 0, SUPER_ROWS) + SUB_ROWS - 1) // SUB_ROWS
        sup_nsub = jnp.where(jnp.arange(n_sup) < sup_end[-1], sup_nsub, 0).astype(jnp.int32)
        row_start = sup_start * SUPER_ROWS
        idx_all = jnp.concatenate([idxp[:, :TOP_K], idxs[:, :TOP_K]], axis=0)
        pos_all = jnp.concatenate([posp[:, :TOP_K], poss[:, :TOP_K]], axis=0)
        dest = (row_start[idx_all] + pos_all).astype(jnp.int32)

        tb = 128
        hw_all = jnp.concatenate([hwp, hws], axis=0)
        xs = _dispatch(hw_all, _dest_blocks(dest, tb), m_pad, tb)
        ysort = _experts(xs, sup_exp, s_ids, sup_nsub, w_up[l], b_up[l], w_down[l], b_down[l])
        yp = _combine(ysort, _dest_blocks(dest[:t], tb), gatep, x1p, mod_p[5], g_post_ffn[l], tb)
        tbs = min(tb, rs)
        ys = _combine(ysort, _dest_blocks(dest[t:], tbs), gates_s, x1s, mod_s[5], g_post_ffn[l], tbs)

        outs["kp"].append(k32.reshape(bp, t, n_heads, hd))
        outs["vp"].append(v32.reshape(bp, t, n_heads, hd))
        outs["lfp"].append(lf_p[:, :n_heads].reshape(bp, t, n_heads))
        outs["cp"].append(tail_p.reshape(bp, CONV_W - 1, dc))
        outs["ks"].append(ks32.reshape(bs, s_new, n_heads, hd))
        outs["vs"].append(vs32.reshape(bs, s_new, n_heads, hd))
        outs["lfs"].append(lf_s[:, :n_heads].reshape(bs, s_new, n_heads))
        outs["cs"].append(u_s.reshape(bs, s_new, dc)[:, s_new - (CONV_W - 1):])

    stack = lambda k: jnp.stack(outs[k])
    return (yp.reshape(bp, t, d), ys.reshape(bs, s_new, d), stack("kp"), stack("vp"), stack("lfp"), stack("cp"),
            stack("ks"), stack("vs"), stack("lfs"), stack("cs"))
```
